```python
import math, functools
import jax, jax.numpy as jnp
from jax import lax
import numpy as np

D_MODEL = 2048
BATCH = 4
SEQ = 2048
DEPTH = 1
DEC_BATCH = 128
DEC_SEQ = 8
PAST_LEN = 2048
PAGE_SIZE = 128

N_HEADS = 8
D_HEAD = 128
D_ATTN = N_HEADS * 2 * D_HEAD
D_SSM = D_MODEL // 2
SSM_GROUP = 16
N_GROUPS = D_SSM // SSM_GROUP
SSM_STATE = 64
Q_BLOCK = 128
POOL_NUM = 5
POOL_DEN = 4
EPS = 1e-6
DT_MIN = 1e-3
DT_MAX = 1e-1
IN_SIZES = (D_ATTN, D_ATTN, D_ATTN, D_ATTN, D_SSM, D_SSM, D_MODEL, D_MODEL)
IN_COLS = sum(IN_SIZES)
IN_SPLITS = tuple(int(s) for s in np.cumsum(IN_SIZES)[:-1])

kernel_name = 'hybrid_diffattn_s5_gated_decoder_step'


def rms_norm(x, g):
    xf = x.astype(jnp.float32)
    r = lax.rsqrt(jnp.mean(xf * xf, axis=-1, keepdims=True) + EPS)
    return (xf * r * g.astype(jnp.float32)).astype(x.dtype)


def lambda_init(layer):
    return 0.8 - 0.6 * math.exp(-0.3 * layer)


def diff_attend(q, k_segs, v_segs, masks, lam):
    B, Lq = q.shape[0], q.shape[1]
    qm = q.reshape(B, Lq, N_HEADS, 2, D_HEAD)
    scores = []
    for k, m in zip(k_segs, masks):
        km = k.reshape(k.shape[0], k.shape[1], N_HEADS, 2, D_HEAD)
        s = jnp.einsum('bqhmd,bkhmd->bhmqk', qm, km, preferred_element_type=jnp.float32) * (D_HEAD ** -0.5)
        if m is not None:
            s = jnp.where(m, s, -jnp.inf)
        scores.append(s)
    s = jnp.concatenate(scores, axis=-1) if len(scores) > 1 else scores[0]
    p = jax.nn.softmax(s, axis=-1)
    p = p[:, :, 0] - lam * p[:, :, 1]
    out = None
    start = 0
    for v in v_segs:
        lk = v.shape[1]
        o = jnp.einsum('bhqk,bkhe->bqhe', p[..., start:start + lk], v, preferred_element_type=jnp.float32)
        out = o if out is None else out + o
        start += lk
    return out


def prompt_attention(q, k, v, lam):
    B, L = q.shape[0], q.shape[1]
    nb = L // Q_BLOCK
    qb = q.reshape(B, nb, Q_BLOCK, N_HEADS, 2 * D_HEAD).transpose(1, 0, 2, 3, 4)
    k_pos = jnp.arange(L)

    def block(args):
        i, q_i = args
        q_pos = i * Q_BLOCK + jnp.arange(Q_BLOCK)
        mask = k_pos[None, :] <= q_pos[:, None]
        return diff_attend(q_i, [k], [v], [mask], lam)

    out = lax.map(block, (jnp.arange(nb), qb))
    return out.transpose(1, 0, 2, 3, 4).reshape(B, L, N_HEADS, 2 * D_HEAD)


def sample_attention(q, k, v, lam, k_past, v_past):
    lq = q.shape[1]
    causal = jnp.tril(jnp.ones((lq, lq), dtype=bool))
    return diff_attend(q, [k_past, k], [v_past, v], [None, causal], lam)


def s5_scan(u, h0, lam_re, lam_im, log_dt, b_re, b_im, c_re, c_im, d_skip):
    bsz, L = u.shape[0], u.shape[1]
    f32 = jnp.float32
    uf = u.astype(f32).reshape(bsz, L, N_GROUPS, SSM_GROUP)
    lam = lax.complex(lam_re.astype(f32), lam_im.astype(f32))
    dt = jnp.exp(log_dt.astype(f32))[:, None]
    lam_bar = jnp.exp(lam * dt)
    b = lax.complex(b_re.astype(f32), b_im.astype(f32))
    b_bar = ((lam_bar - 1.0) / lam)[..., None] * b
    c = lax.complex(c_re.astype(f32), c_im.astype(f32))
    bu = jnp.einsum('gpc,blgc->blgp', b_bar, uf.astype(jnp.complex64))
    bu = bu.at[:, 0].add(lam_bar[None] * h0)
    a = jnp.broadcast_to(lam_bar, bu.shape)

    def combine(e1, e2):
        a1, b1 = e1
        a2, b2 = e2
        return a1 * a2, a2 * b1 + b2

    _, h = lax.associative_scan(combine, (a, bu), axis=1)
    y = jnp.einsum('gcp,blgp->blgc', c, h).real + d_skip.astype(f32).reshape(N_GROUPS, SSM_GROUP) * uf
    return y.reshape(bsz, L, D_SSM), h[:, -1]


def mixer_layer(x, c, h0, attend, lam_init, p):
    B, L = x.shape[0], x.shape[1]
    mod = (c @ p['w_ada'] + p['b_ada'])[:, None, :]
    shift, scale, gate = jnp.split(mod, 3, axis=-1)
    h = rms_norm(x, p['g_pre']) * (1.0 + scale) + shift
    proj = h @ p['w_in']
    q, k, v, a_gate, u, s_gate, g_a, g_s = jnp.split(proj, IN_SPLITS, axis=-1)
    q = q.reshape(B, L, N_HEADS, 2 * D_HEAD)
    k = k.reshape(B, L, N_HEADS, 2 * D_HEAD)
    v = v.reshape(B, L, N_HEADS, 2 * D_HEAD)
    f32 = jnp.float32
    lam = (jnp.exp(jnp.sum(p['lq1'].astype(f32) * p['lk1'].astype(f32)))
           - jnp.exp(jnp.sum(p['lq2'].astype(f32) * p['lk2'].astype(f32))) + lam_init)
    attn = attend(q, k, v, lam)
    attn = rms_norm(attn, p['g_subln']) * (1.0 - lam_init)
    attn = attn.reshape(B, L, D_ATTN).astype(x.dtype) * jax.nn.silu(a_gate)
    y_s, h_last = s5_scan(u, h0, p['lam_re'], p['lam_im'], p['log_dt'], p['b_re'], p['b_im'],
                          p['c_re'], p['c_im'], p['d_skip'])
    y_s = jax.nn.gelu(y_s).astype(x.dtype)
    y_s = y_s * jax.nn.sigmoid(y_s @ p['w_glu'] + p['b_glu'])
    y_s = y_s * jax.nn.silu(s_gate)
    merged = jax.nn.sigmoid(g_a) * (attn @ p['w_proj_a']) + jax.nn.sigmoid(g_s) * (y_s @ p['w_proj_s'])
    o = merged @ p['w_out']
    y = x + gate * rms_norm(o, p['g_post'])
    h_state = jnp.stack([h_last.real, h_last.imag], axis=-1).astype(x.dtype)
    return y, k, v, h_state


def setup_inputs(seed: int = 0) -> dict:
    key = jax.random.key(seed)
    ks = jax.random.split(key, 40)
    f32 = jnp.float32
    n_pages = PAST_LEN // PAGE_SIZE
    n_phys = (DEC_BATCH * n_pages * POOL_NUM) // POOL_DEN
    nrm = lambda k, shape, s: jax.random.normal(k, shape, f32) * s
    page_table = jax.random.permutation(ks[5], n_phys)[: DEC_BATCH * n_pages]
    page_table = page_table.reshape(DEC_BATCH, n_pages).astype(jnp.int32)
    lam_im0 = math.pi * jnp.arange(SSM_STATE, dtype=f32)
    return {
        'x_prompt': nrm(ks[0], (BATCH, SEQ, D_MODEL), 1.0),
        'x_sample': nrm(ks[1], (DEC_BATCH, DEC_SEQ, D_MODEL), 1.0),
        'cache_k': nrm(ks[2], (DEPTH, n_phys, PAGE_SIZE, N_HEADS, 2 * D_HEAD), 1.0),
        'cache_v': nrm(ks[3], (DEPTH, n_phys, PAGE_SIZE, N_HEADS, 2 * D_HEAD), 1.0),
        'state_ssm': nrm(ks[4], (DEPTH, DEC_BATCH, N_GROUPS, SSM_STATE, 2), 0.5),
        'page_table': page_table,
        'c_prompt': nrm(ks[6], (BATCH, D_MODEL), 1.0),
        'c_sample': nrm(ks[7], (DEC_BATCH, D_MODEL), 1.0),
        'w_ada': nrm(ks[8], (DEPTH, D_MODEL, 3 * D_MODEL), 0.5 * D_MODEL ** -0.5),
        'b_ada': nrm(ks[9], (DEPTH, 3 * D_MODEL), 0.01),
        'g_pre': 1.0 + nrm(ks[10], (DEPTH, D_MODEL), 0.02),
        'g_post': 1.0 + nrm(ks[11], (DEPTH, D_MODEL), 0.02),
        'w_in': nrm(ks[12], (DEPTH, D_MODEL, IN_COLS), D_MODEL ** -0.5),
        'lambda_q1': nrm(ks[13], (DEPTH, D_HEAD), 0.1),
        'lambda_k1': nrm(ks[14], (DEPTH, D_HEAD), 0.1),
        'lambda_q2': nrm(ks[15], (DEPTH, D_HEAD), 0.1),
        'lambda_k2': nrm(ks[16], (DEPTH, D_HEAD), 0.1),
        'g_subln': 1.0 + nrm(ks[17], (DEPTH, 2 * D_HEAD), 0.02),
        'ssm_lambda_re': -0.5 + nrm(ks[18], (DEPTH, N_GROUPS, SSM_STATE), 0.01),
        'ssm_lambda_im': lam_im0 + nrm(ks[19], (DEPTH, N_GROUPS, SSM_STATE), 0.01),
        'ssm_log_dt': jax.random.uniform(ks[20], (DEPTH, N_GROUPS), f32, math.log(DT_MIN), math.log(DT_MAX)),
        'ssm_b_re': nrm(ks[21], (DEPTH, N_GROUPS, SSM_STATE, SSM_GROUP), (2 * SSM_GROUP) ** -0.5),
        'ssm_b_im': nrm(ks[22], (DEPTH, N_GROUPS, SSM_STATE, SSM_GROUP), (2 * SSM_GROUP) ** -0.5),
        'ssm_c_re': nrm(ks[23], (DEPTH, N_GROUPS, SSM_GROUP, SSM_STATE), (2 * SSM_STATE) ** -0.5),
        'ssm_c_im': nrm(ks[24], (DEPTH, N_GROUPS, SSM_GROUP, SSM_STATE), (2 * SSM_STATE) ** -0.5),
        'ssm_d': nrm(ks[25], (DEPTH, D_SSM), 1.0),
        'w_glu': nrm(ks[26], (DEPTH, D_SSM, D_SSM), D_SSM ** -0.5),
        'b_glu': nrm(ks[27], (DEPTH, D_SSM), 0.01),
        'w_proj_a': nrm(ks[28], (DEPTH, D_ATTN, D_MODEL), D_ATTN ** -0.5),
        'w_proj_s': nrm(ks[29], (DEPTH, D_SSM, D_MODEL), D_SSM ** -0.5),
        'w_out': nrm(ks[30], (DEPTH, D_MODEL, D_MODEL), D_MODEL ** -0.5),
    }


def reference(x_prompt, x_sample, cache_k, cache_v, state_ssm, page_table, c_prompt, c_sample,
              w_ada, b_ada, g_pre, g_post, w_in, lambda_q1, lambda_k1, lambda_q2, lambda_k2,
              g_subln, ssm_lambda_re, ssm_lambda_im, ssm_log_dt, ssm_b_re, ssm_b_im,
              ssm_c_re, ssm_c_im, ssm_d, w_glu, b_glu, w_proj_a, w_proj_s, w_out):
    yp, ys = x_prompt, x_sample
    kp_all, vp_all, hp_all, ks_all, vs_all, hs_all = [], [], [], [], [], []
    for l in range(DEPTH):
        p = {'w_ada': w_ada[l], 'b_ada': b_ada[l], 'g_pre': g_pre[l], 'g_post': g_post[l],
             'w_in': w_in[l], 'lq1': lambda_q1[l], 'lk1': lambda_k1[l], 'lq2': lambda_q2[l],
             'lk2': lambda_k2[l], 'g_subln': g_subln[l], 'lam_re': ssm_lambda_re[l],
             'lam_im': ssm_lambda_im[l], 'log_dt': ssm_log_dt[l], 'b_re': ssm_b_re[l],
             'b_im': ssm_b_im[l], 'c_re': ssm_c_re[l], 'c_im': ssm_c_im[l], 'd_skip': ssm_d[l],
             'w_glu': w_glu[l], 'b_glu': b_glu[l], 'w_proj_a': w_proj_a[l],
             'w_proj_s': w_proj_s[l], 'w_out': w_out[l]}
        lam0 = lambda_init(l)
        h0_p = jnp.zeros((yp.shape[0], N_GROUPS, SSM_STATE), jnp.complex64)
        yp, kp, vp, hp = mixer_layer(yp, c_prompt, h0_p, prompt_attention, lam0, p)
        k_past = cache_k[l, page_table].reshape(ys.shape[0], -1, N_HEADS, 2 * D_HEAD)
        v_past = cache_v[l, page_table].reshape(ys.shape[0], -1, N_HEADS, 2 * D_HEAD)
        s0 = state_ssm[l].astype(jnp.float32)
        h0_s = lax.complex(s0[..., 0], s0[..., 1])
        attend_s = functools.partial(sample_attention, k_past=k_past, v_past=v_past)
        ys, ks_, vs_, hs = mixer_layer(ys, c_sample, h0_s, attend_s, lam0, p)
        kp_all.append(kp)
        vp_all.append(vp)
        hp_all.append(hp)
        ks_all.append(ks_)
        vs_all.append(vs_)
        hs_all.append(hs)
    k_prompt = jnp.stack(kp_all, axis=0)
    v_prompt = jnp.stack(vp_all, axis=0)
    ssm_prompt = jnp.stack(hp_all, axis=0)
    k_sample = jnp.stack(ks_all, axis=0)
    v_sample = jnp.stack(vs_all, axis=0)
    ssm_sample = jnp.stack(hs_all, axis=0)
    return (yp, ys, k_prompt, v_prompt, ssm_prompt, k_sample, v_sample, ssm_sample)
```

```python
import functools
import math

import jax
import jax.numpy as jnp
from jax import lax
from jax.experimental import pallas as pl
from jax.experimental.pallas import tpu as pltpu

F32 = jnp.float32
BF16 = jnp.bfloat16

EPS = 1e-6
N_HEADS = 8
D_HEAD = 128
HEAD_W = 2 * D_HEAD
SSM_GROUP = 16
SSM_STATE = 64
SUBLANES = 8
LANES = 128
VMEM_LIMIT = 52 * 1024 * 1024

_NT = (((1,), (1,)), ((), ()))


def _params(n_axes, vmem=VMEM_LIMIT):
    return pltpu.CompilerParams(dimension_semantics=("arbitrary",) * n_axes, vmem_limit_bytes=vmem)


def _lambda_full(lq1, lk1, lq2, lk2, lam_init):
    a = jnp.exp(jnp.sum(lq1[...] * lk1[...], axis=-1, keepdims=True))
    b = jnp.exp(jnp.sum(lq2[...] * lk2[...], axis=-1, keepdims=True))
    return a - b + lam_init


def _mod_body(c_ref, w_ref, b_ref, o_ref):
    o_ref[...] = jnp.dot(c_ref[...].astype(BF16), w_ref[...].astype(BF16),
                         preferred_element_type=F32) + b_ref[...]


def _modulation(c_all, w_ada, b_ada, tn=512):
    rows, d = c_all.shape
    n = w_ada.shape[1]
    return pl.pallas_call(
        _mod_body,
        grid=(n // tn,),
        in_specs=[pl.BlockSpec((rows, d), lambda j: (0, 0)),
                  pl.BlockSpec((d, tn), lambda j: (0, j)),
                  pl.BlockSpec((1, tn), lambda j: (0, j))],
        out_specs=pl.BlockSpec((rows, tn), lambda j: (0, j)),
        out_shape=jax.ShapeDtypeStruct((rows, n), F32),
        compiler_params=_params(1),
        name="adaln_mod",
    )(c_all, w_ada, b_ada.reshape(1, n))


def _inproj_body(x_ref, sc_ref, sh_ref, g_ref, w_ref, q_ref, k_ref, v_ref, r_ref, h_scr,
                 *, n_qkv, q_scale):
    j = pl.program_id(1)

    @pl.when(j == 0)
    def _():
        x = x_ref[...]
        r = lax.rsqrt(jnp.mean(x * x, axis=-1, keepdims=True) + EPS)
        h = x * r * g_ref[...]
        h = h * (1.0 + sc_ref[...]) + sh_ref[...]
        h_scr[...] = h.reshape(h_scr.shape).astype(BF16)

    acc = jnp.dot(h_scr[...], w_ref[...], preferred_element_type=F32)

    @pl.when(j < n_qkv)
    def _():
        q_ref[...] = (acc * q_scale).astype(q_ref.dtype)

    @pl.when((j >= n_qkv) & (j < 2 * n_qkv))
    def _():
        k_ref[...] = acc

    @pl.when((j >= 2 * n_qkv) & (j < 3 * n_qkv))
    def _():
        v_ref[...] = acc

    @pl.when(j >= 3 * n_qkv)
    def _():
        r_ref[...] = acc


def _in_projection(x3, mod3, g_pre, w_in_bf, *, bb, ll, q_dtype, tn=512):
    bx, lx, d = x3.shape
    n_cols = w_in_bf.shape[1]
    d_attn = N_HEADS * HEAD_W
    rows = bx * lx
    tm = bb * ll
    tiles_per_b = lx // ll
    n_row_tiles = (bx // bb) * tiles_per_b
    n_qkv = d_attn // tn
    n_rest = (n_cols - 3 * d_attn) // tn
    n_steps = n_cols // tn

    def out_map(lo, n):
        return lambda i, j: (i, jnp.clip(j - lo, 0, n - 1))

    body = functools.partial(_inproj_body, n_qkv=n_qkv, q_scale=D_HEAD ** -0.5)
    return pl.pallas_call(
        body,
        grid=(n_row_tiles, n_steps),
        in_specs=[pl.BlockSpec((bb, ll, d), lambda i, j: (i // tiles_per_b, i % tiles_per_b, 0)),
                  pl.BlockSpec((bb, 1, d), lambda i, j: (i // tiles_per_b, 0, 1)),
                  pl.BlockSpec((bb, 1, d), lambda i, j: (i // tiles_per_b, 0, 0)),
                  pl.BlockSpec((1, d), lambda i, j: (0, 0)),
                  pl.BlockSpec((d, tn), lambda i, j: (0, j))],
        out_specs=[pl.BlockSpec((tm, tn), out_map(0, n_qkv)),
                   pl.BlockSpec((tm, tn), out_map(n_qkv, n_qkv)),
                   pl.BlockSpec((tm, tn), out_map(2 * n_qkv, n_qkv)),
                   pl.BlockSpec((tm, tn), out_map(3 * n_qkv, n_rest))],
        out_shape=[jax.ShapeDtypeStruct((rows, d_attn), q_dtype),
                   jax.ShapeDtypeStruct((rows, d_attn), F32),
                   jax.ShapeDtypeStruct((rows, d_attn), F32),
                   jax.ShapeDtypeStruct((rows, n_cols - 3 * d_attn), F32)],
        scratch_shapes=[pltpu.VMEM((tm, d), BF16)],
        compiler_params=_params(2),
        name="in_projection",
    )(x3, mod3, mod3, g_pre.reshape(1, d), w_in_bf)


def _softmax_update(s, v_bf, m_ref, l_ref, acc_ref, idx):
    m_prev = m_ref[idx]
    m_new = jnp.maximum(m_prev, jnp.max(s, axis=-1, keepdims=True))
    alpha = jnp.exp(m_prev - m_new)
    p = jnp.exp(s - m_new)
    l_ref[idx] = alpha * l_ref[idx] + jnp.sum(p, axis=-1, keepdims=True)
    acc_ref[idx] = alpha * acc_ref[idx] + jnp.dot(p.astype(BF16), v_bf, preferred_element_type=F32)
    m_ref[idx] = m_new


def _subln_gate(o, g_sub, a_gate, lam_init):
    r = lax.rsqrt(jnp.mean(o * o, axis=-1, keepdims=True) + EPS)
    return (o * r * g_sub * (1.0 - lam_init)) * (a_gate * jax.nn.sigmoid(a_gate))


def _pattn_body(lq1, lk1, lq2, lk2, gs_ref, q_ref, k_ref, v_ref, a_ref, o_ref,
                kb, vb, m_scr, l_scr, acc_scr, *, tq, lam_init):
    i = pl.program_id(2)

    @pl.when(i == 0)
    def _():
        kb[...] = k_ref[0].astype(BF16)
        vb[...] = v_ref[0].astype(BF16)

    q = q_ref[0]
    m_scr[...] = jnp.full(m_scr.shape, -jnp.inf, F32)
    l_scr[...] = jnp.zeros(l_scr.shape, F32)
    acc_scr[...] = jnp.zeros(acc_scr.shape, F32)

    def step(j, causal):
        off = pl.multiple_of(j * tq, tq)
        kj = kb[pl.ds(off, tq), :]
        vj = vb[pl.ds(off, tq), :]
        for half in range(2):
            sl = slice(half * D_HEAD, (half + 1) * D_HEAD)
            s = lax.dot_general(q[:, sl], kj[:, sl], _NT, preferred_element_type=F32)
            if causal:
                rows = lax.broadcasted_iota(jnp.int32, s.shape, 0)
                cols = lax.broadcasted_iota(jnp.int32, s.shape, 1)
                s = jnp.where(cols <= rows, s, -jnp.inf)
            _softmax_update(s, vj, m_scr, l_scr, acc_scr, half)

    def loop_body(j, carry):
        step(j, False)
        return carry

    lax.fori_loop(0, i, loop_body, 0)
    step(i, True)

    lam = _lambda_full(lq1, lk1, lq2, lk2, lam_init)
    o = acc_scr[0] / l_scr[0] - lam * (acc_scr[1] / l_scr[1])
    o_ref[0] = _subln_gate(o, gs_ref[...], a_ref[0], lam_init).astype(o_ref.dtype)


def _prompt_attention(q3, k3, v3, rest3, lam_vecs, g_subln, lam_init, tq=256):
    b, l, _ = q3.shape
    vec = pl.BlockSpec((1, D_HEAD), lambda bi, h, i: (0, 0))
    body = functools.partial(_pattn_body, tq=tq, lam_init=lam_init)
    return pl.pallas_call(
        body,
        grid=(b, N_HEADS, l // tq),
        in_specs=[vec, vec, vec, vec,
                  pl.BlockSpec((1, HEAD_W), lambda bi, h, i: (0, 0)),
                  pl.BlockSpec((1, tq, HEAD_W), lambda bi, h, i: (bi, i, h)),
                  pl.BlockSpec((1, l, HEAD_W), lambda bi, h, i: (bi, 0, h)),
                  pl.BlockSpec((1, l, HEAD_W), lambda bi, h, i: (bi, 0, h)),
                  pl.BlockSpec((1, tq, HEAD_W), lambda bi, h, i: (bi, i, h))],
        out_specs=pl.BlockSpec((1, tq, HEAD_W), lambda bi, h, i: (bi, i, h)),
        out_shape=jax.ShapeDtypeStruct((b, l, N_HEADS * HEAD_W), BF16),
        scratch_shapes=[pltpu.VMEM((l, HEAD_W), BF16), pltpu.VMEM((l, HEAD_W), BF16),
                        pltpu.VMEM((2, tq, 1), F32), pltpu.VMEM((2, tq, 1), F32),
                        pltpu.VMEM((2, tq, HEAD_W), F32)],
        compiler_params=_params(3),
        name="prompt_attention",
    )(*lam_vecs, g_subln.reshape(1, HEAD_W), q3, k3, v3, rest3)


def _sattn_body(pt_ref, lq1, lk1, lq2, lk2, gs_ref, q_ref, kn_ref, vn_ref, a_ref, kc_ref, vc_ref,
                o_ref, qbd, m_scr, l_scr, acc_scr, kpad, vpad, *, n_pages, page, lq, lam_init):
    del pt_ref
    b = pl.program_id(0)
    j = pl.program_id(1)

    @pl.when((b == 0) & (j == 0))
    def _():
        kpad[...] = jnp.zeros(kpad.shape, F32)
        vpad[...] = jnp.zeros(vpad.shape, F32)
        qbd[...] = jnp.zeros(qbd.shape, F32)

    @pl.when(j == 0)
    def _():
        m_scr[...] = jnp.full(m_scr.shape, -jnp.inf, F32)
        l_scr[...] = jnp.zeros(l_scr.shape, F32)
        acc_scr[...] = jnp.zeros(acc_scr.shape, F32)
        q = q_ref[0]
        for h in range(N_HEADS):
            qbd[h, 0:lq, 0:D_HEAD] = q[:, h * HEAD_W:h * HEAD_W + D_HEAD]
            qbd[h, lq:2 * lq, D_HEAD:HEAD_W] = q[:, h * HEAD_W + D_HEAD:(h + 1) * HEAD_W]

    def process(k_of_head, v_of_head, mask):
        for h in range(N_HEADS):
            s = lax.dot_general(qbd[h].astype(BF16), k_of_head(h).astype(BF16), _NT,
                                preferred_element_type=F32)
            if mask is not None:
                s = jnp.where(mask, s, -jnp.inf)
            _softmax_update(s, v_of_head(h).astype(BF16), m_scr, l_scr, acc_scr, h)

    process(lambda h: kc_ref[0, 0, :, h, :], lambda h: vc_ref[0, 0, :, h, :], None)

    @pl.when(j == n_pages - 1)
    def _():
        kpad[0:lq, :] = kn_ref[0]
        vpad[0:lq, :] = vn_ref[0]
        rows = lax.broadcasted_iota(jnp.int32, (2 * lq, page), 0)
        cols = lax.broadcasted_iota(jnp.int32, (2 * lq, page), 1)
        causal = cols <= jnp.where(rows >= lq, rows - lq, rows)
        process(lambda h: kpad[:, h * HEAD_W:(h + 1) * HEAD_W],
                lambda h: vpad[:, h * HEAD_W:(h + 1) * HEAD_W], causal)
        lam = _lambda_full(lq1, lk1, lq2, lk2, lam_init)
        for h in range(N_HEADS):
            o = acc_scr[h] / l_scr[h]
            o = o[0:lq] - lam * o[lq:2 * lq]
            sl = slice(h * HEAD_W, (h + 1) * HEAD_W)
            o_ref[0, :, sl] = _subln_gate(o, gs_ref[...], a_ref[0, :, sl], lam_init)


def _sample_attention(q3, kn3, vn3, rest3, cache_k, cache_v, layer, page_table, lam_vecs, g_subln, lam_init):
    b, lq, d_attn = q3.shape
    n_pages = page_table.shape[1]
    page = cache_k.shape[2]
    vec = pl.BlockSpec((1, D_HEAD), lambda bi, j, pt: (0, 0))
    tok = pl.BlockSpec((1, lq, d_attn), lambda bi, j, pt: (bi, 0, 0))
    cache = pl.BlockSpec((1, 1, page, N_HEADS, HEAD_W), lambda bi, j, pt: (layer, pt[bi * n_pages + j], 0, 0, 0))
    body = functools.partial(_sattn_body, n_pages=n_pages, page=page, lq=lq, lam_init=lam_init)
    grid_spec = pltpu.PrefetchScalarGridSpec(
        num_scalar_prefetch=1,
        grid=(b, n_pages),
        in_specs=[vec, vec, vec, vec,
                  pl.BlockSpec((1, HEAD_W), lambda bi, j, pt: (0, 0)),
                  tok, tok, tok, tok, cache, cache],
        out_specs=tok,
        scratch_shapes=[pltpu.VMEM((N_HEADS, 2 * lq, HEAD_W), F32),
                        pltpu.VMEM((N_HEADS, 2 * lq, 1), F32),
                        pltpu.VMEM((N_HEADS, 2 * lq, 1), F32),
                        pltpu.VMEM((N_HEADS, 2 * lq, HEAD_W), F32),
                        pltpu.VMEM((page, d_attn), F32),
                        pltpu.VMEM((page, d_attn), F32)])
    return pl.pallas_call(
        body,
        grid_spec=grid_spec,
        out_shape=jax.ShapeDtypeStruct((b, lq, d_attn), F32),
        compiler_params=_params(2),
        name="sample_attention",
    )(page_table.reshape(-1), *lam_vecs, g_subln.reshape(1, HEAD_W), q3, kn3, vn3, rest3, cache_k, cache_v)


def _ssm_prep_body(lr_ref, li_ref, ldt_ref, br_ref, bi_ref, lbr_ref, lbi_ref, bbr_ref, bbi_ref):
    lr = lr_ref[...]
    li = li_ref[...]
    dt = jnp.exp(ldt_ref[...])
    mag = jnp.exp(lr * dt)
    lbr = mag * jnp.cos(li * dt)
    lbi = mag * jnp.sin(li * dt)
    den = lr * lr + li * li
    fr = ((lbr - 1.0) * lr + lbi * li) / den
    fi = (lbi * lr - (lbr - 1.0) * li) / den
    lbr_ref[...] = lbr
    lbi_ref[...] = lbi
    br = br_ref[...]
    bi = bi_ref[...]
    bbr_ref[...] = fr[:, None, :] * br - fi[:, None, :] * bi
    bbi_ref[...] = fr[:, None, :] * bi + fi[:, None, :] * br


def _ssm_prep(lam_re, lam_im, log_dt, b_re_t, b_im_t):
    g, p = lam_re.shape
    c = b_re_t.shape[1]
    gp = jax.ShapeDtypeStruct((g, p), F32)
    gcp = jax.ShapeDtypeStruct((g, c, p), F32)
    return pl.pallas_call(_ssm_prep_body, out_shape=[gp, gp, gcp, gcp], name="ssm_discretise")(
        lam_re, lam_im, log_dt.reshape(g, 1), b_re_t, b_im_t)


def _blockdiag(w):
    n, g, a, b = w.shape
    eye = jnp.eye(g, dtype=w.dtype)
    return (w[:, :, :, None, :] * eye[None, :, None, :, None]).reshape(n, g * a, g * b)


def _s5p_body(u_ref, wb_ref, wc_ref, a_ref, bs_ref, d_ref, y_ref, hl_ref, lhs, hs, zs, hstate,
              *, nb, t_chunk, n_blk, cw, sw):
    c = pl.program_id(0)
    half = SUBLANES // 2

    @pl.when(c == 0)
    def _():
        lhs[...] = jnp.zeros(lhs.shape, F32)
        hstate[...] = jnp.zeros(hstate.shape, F32)

    tpc = cw // LANES
    for b in range(nb):
        for k in range(n_blk * tpc):
            cb, w = divmod(k, tpc)
            ub = u_ref[b, :, k * LANES:(k + 1) * LANES]
            lhs[2 * cb * tpc + w, pl.ds(b, t_chunk, stride=SUBLANES), :] = ub
            lhs[(2 * cb + 1) * tpc + w, pl.ds(half + b, t_chunk, stride=SUBLANES), :] = ub
    for cb in range(n_blk):
        lhs_cb = jnp.concatenate([lhs[2 * cb * tpc + i] for i in range(2 * tpc)], axis=1)
        hs[:, cb * sw:(cb + 1) * sw] = jnp.dot(lhs_cb.astype(BF16), wb_ref[cb], preferred_element_type=F32)

    for cb in range(n_blk):
        cols = slice(cb * sw, (cb + 1) * sw)
        a = a_ref[:, cols]
        bsw = bs_ref[:, cols]

        def scan_step(t, h, cols=cols, a=a, bsw=bsw):
            r = pl.multiple_of(t * SUBLANES, SUBLANES)
            h = a * h + bsw * pltpu.roll(h, half, axis=0) + hs[pl.ds(r, SUBLANES), cols]
            hs[pl.ds(r, SUBLANES), cols] = h
            return h

        hstate[:, cols] = lax.fori_loop(0, t_chunk, scan_step, hstate[:, cols], unroll=8)

    for cb in range(n_blk):
        z = jnp.dot(hs[:, cb * sw:(cb + 1) * sw].astype(BF16), wc_ref[cb], preferred_element_type=F32)
        for i in range(2 * tpc):
            zs[i] = z[:, i * LANES:(i + 1) * LANES]
        for b in range(nb):
            for w in range(tpc):
                ccols = slice(cb * cw + w * LANES, cb * cw + (w + 1) * LANES)
                y = (zs[w, pl.ds(b, t_chunk, stride=SUBLANES), :]
                     - zs[tpc + w, pl.ds(half + b, t_chunk, stride=SUBLANES), :])
                y_ref[b, :, ccols] = y + d_ref[:, ccols] * u_ref[b, :, ccols]

    @pl.when(c == pl.num_programs(0) - 1)
    def _():
        hl_ref[...] = hstate[...]


def _s5_prompt(rest3, u_col_block, wb, wc, a_rows, bs_rows, d_skip, t_chunk=64):
    nb, l, _ = rest3.shape
    n_blk, two_cw, sw = wb.shape
    cw = two_cw // 2
    d_ssm = n_blk * cw
    n_state = n_blk * sw
    assert 2 * nb == SUBLANES
    body = functools.partial(_s5p_body, nb=nb, t_chunk=t_chunk, n_blk=n_blk, cw=cw, sw=sw)
    rows = SUBLANES * t_chunk
    return pl.pallas_call(
        body,
        grid=(l // t_chunk,),
        in_specs=[pl.BlockSpec((nb, t_chunk, d_ssm), lambda c: (0, c, u_col_block)),
                  pl.BlockSpec(wb.shape, lambda c: (0, 0, 0)),
                  pl.BlockSpec(wc.shape, lambda c: (0, 0, 0)),
                  pl.BlockSpec((SUBLANES, n_state), lambda c: (0, 0)),
                  pl.BlockSpec((SUBLANES, n_state), lambda c: (0, 0)),
                  pl.BlockSpec((1, d_ssm), lambda c: (0, 0))],
        out_specs=[pl.BlockSpec((nb, t_chunk, d_ssm), lambda c: (0, c, 0)),
                   pl.BlockSpec((SUBLANES, n_state), lambda c: (0, 0))],
        out_shape=[jax.ShapeDtypeStruct((nb, l, d_ssm), F32),
                   jax.ShapeDtypeStruct((SUBLANES, n_state), F32)],
        scratch_shapes=[pltpu.VMEM((2 * d_ssm // LANES, rows, LANES), F32),
                        pltpu.VMEM((rows, n_state), F32),
                        pltpu.VMEM((2 * cw // LANES, rows, LANES), F32),
                        pltpu.VMEM((SUBLANES, n_state), F32)],
        compiler_params=_params(1),
        name="s5_scan_prompt",
    )(rest3, wb, wc, a_rows, bs_rows, d_skip.reshape(1, d_ssm))


def _s5s_body(u_ref, h0r_ref, h0i_ref, wb_ref, wc_ref, lr_ref, li_ref, d_ref, y_ref, hlr_ref, hli_ref,
              hre, him, *, bt, lq, n_blk, cw, sw):
    tps = sw // LANES
    ub = u_ref[...].astype(BF16)
    for cb in range(n_blk):
        ucb = ub[:, cb * cw:(cb + 1) * cw]
        bre = jnp.dot(ucb, wb_ref[cb, 0:cw, :], preferred_element_type=F32)
        bim = jnp.dot(ucb, wb_ref[cb, cw:2 * cw, :], preferred_element_type=F32)
        for i in range(tps):
            hre[cb * tps + i] = bre[:, i * LANES:(i + 1) * LANES]
            him[cb * tps + i] = bim[:, i * LANES:(i + 1) * LANES]

    def tile_body(r, carry):
        r8 = pl.multiple_of(r * SUBLANES, SUBLANES)
        base = r8 * lq
        for k in range(n_blk * tps):
            cols = slice(k * LANES, (k + 1) * LANES)
            lr = lr_ref[:, cols]
            li = li_ref[:, cols]
            h_r = h0r_ref[pl.ds(r8, SUBLANES), cols]
            h_i = h0i_ref[pl.ds(r8, SUBLANES), cols]
            for t in range(lq):
                rows = pl.ds(base + t, SUBLANES, stride=lq)
                n_r = lr * h_r - li * h_i + hre[k, rows, :]
                n_i = lr * h_i + li * h_r + him[k, rows, :]
                h_r, h_i = n_r, n_i
                hre[k, rows, :] = h_r
                him[k, rows, :] = h_i
            hlr_ref[pl.ds(r8, SUBLANES), cols] = h_r
            hli_ref[pl.ds(r8, SUBLANES), cols] = h_i
        return carry

    lax.fori_loop(0, bt // SUBLANES, tile_body, 0)

    for cb in range(n_blk):
        ccols = slice(cb * cw, (cb + 1) * cw)
        h_r = jnp.concatenate([hre[cb * tps + i] for i in range(tps)], axis=1)
        h_i = jnp.concatenate([him[cb * tps + i] for i in range(tps)], axis=1)
        y = (jnp.dot(h_r.astype(BF16), wc_ref[cb, :, 0:cw], preferred_element_type=F32)
             - jnp.dot(h_i.astype(BF16), wc_ref[cb, :, cw:2 * cw], preferred_element_type=F32))
        y_ref[:, ccols] = y + d_ref[:, ccols] * u_ref[:, ccols]


def _s5_sample(rest2, u_col_block, h0r, h0i, wb, wc, lbr, lbi, d_skip, lq, bt=32):
    rows = rest2.shape[0]
    nbatch = rows // lq
    n_blk, two_cw, sw = wb.shape
    cw = two_cw // 2
    d_ssm = n_blk * cw
    n_state = n_blk * sw
    body = functools.partial(_s5s_body, bt=bt, lq=lq, n_blk=n_blk, cw=cw, sw=sw)
    st = pl.BlockSpec((bt, n_state), lambda i: (i, 0))
    return pl.pallas_call(
        body,
        grid=(nbatch // bt,),
        in_specs=[pl.BlockSpec((bt * lq, d_ssm), lambda i: (i, u_col_block)),
                  st, st,
                  pl.BlockSpec(wb.shape, lambda i: (0, 0, 0)),
                  pl.BlockSpec(wc.shape, lambda i: (0, 0, 0)),
                  pl.BlockSpec((1, n_state), lambda i: (0, 0)),
                  pl.BlockSpec((1, n_state), lambda i: (0, 0)),
                  pl.BlockSpec((1, d_ssm), lambda i: (0, 0))],
        out_specs=[pl.BlockSpec((bt * lq, d_ssm), lambda i: (i, 0)), st, st],
        out_shape=[jax.ShapeDtypeStruct((rows, d_ssm), F32),
                   jax.ShapeDtypeStruct((nbatch, n_state), F32),
                   jax.ShapeDtypeStruct((nbatch, n_state), F32)],
        scratch_shapes=[pltpu.VMEM((n_state // LANES, bt * lq, LANES), F32),
                        pltpu.VMEM((n_state // LANES, bt * lq, LANES), F32)],
        compiler_params=_params(1),
        name="s5_scan_sample",
    )(rest2, h0r, h0i, wb, wc, lbr, lbi, d_skip.reshape(1, d_ssm))


def _merge_body(attn_ref, y_ref, sg_ref, ga_ref, gs_ref, wglu_ref, bglu_ref, wpa_ref, wps_ref, o_ref):
    ys = jax.nn.gelu(y_ref[...])
    glu = jnp.dot(ys.astype(BF16), wglu_ref[...], preferred_element_type=F32) + bglu_ref[...]
    ys = ys * jax.nn.sigmoid(glu)
    sg = sg_ref[...]
    ys = ys * (sg * jax.nn.sigmoid(sg))
    pa = jnp.dot(attn_ref[...].astype(BF16), wpa_ref[...], preferred_element_type=F32)
    ps = jnp.dot(ys.astype(BF16), wps_ref[...], preferred_element_type=F32)
    merged = jax.nn.sigmoid(ga_ref[...]) * pa + jax.nn.sigmoid(gs_ref[...]) * ps
    o_ref[...] = merged.astype(o_ref.dtype)


def _resident(shape):
    return pl.BlockSpec(shape, lambda i: (0,) * len(shape), pipeline_mode=pl.Buffered(1))


def _merge(attn2, y2, rest2, w_glu_bf, b_glu, w_pa_bf, w_ps_bf, tm=256):
    rows, d_attn = attn2.shape
    tm = min(tm, rows)
    d_ssm = y2.shape[1]
    d = w_pa_bf.shape[1]
    sg_blk = (d_attn + d_ssm) // d_ssm
    ga_blk = (d_attn + 2 * d_ssm) // d
    return pl.pallas_call(
        _merge_body,
        grid=(rows // tm,),
        in_specs=[pl.BlockSpec((tm, d_attn), lambda i: (i, 0)),
                  pl.BlockSpec((tm, d_ssm), lambda i: (i, 0)),
                  pl.BlockSpec((tm, d_ssm), lambda i: (i, sg_blk)),
                  pl.BlockSpec((tm, d), lambda i: (i, ga_blk)),
                  pl.BlockSpec((tm, d), lambda i: (i, ga_blk + 1)),
                  _resident(w_glu_bf.shape), _resident((1, d_ssm)),
                  _resident(w_pa_bf.shape), _resident(w_ps_bf.shape)],
        out_specs=pl.BlockSpec((tm, d), lambda i: (i, 0)),
        out_shape=jax.ShapeDtypeStruct((rows, d), BF16),
        compiler_params=_params(1),
        name="glu_merge",
    )(attn2, y2, rest2, rest2, rest2, w_glu_bf, b_glu.reshape(1, d_ssm), w_pa_bf, w_ps_bf)


def _out_body(m_ref, x_ref, gate_ref, gp_ref, w_ref, o_ref):
    o = jnp.dot(m_ref[...], w_ref[...], preferred_element_type=F32)
    r = lax.rsqrt(jnp.mean(o * o, axis=-1, keepdims=True) + EPS)
    o = (o * r * gp_ref[...]).reshape(x_ref.shape)
    o_ref[...] = x_ref[...] + gate_ref[...] * o


def _out_projection(merged2, x3, mod3, g_post, w_out_bf, *, bb, ll):
    bx, lx, d = x3.shape
    tiles_per_b = lx // ll
    n_row_tiles = (bx // bb) * tiles_per_b
    xspec = pl.BlockSpec((bb, ll, d), lambda i: (i // tiles_per_b, i % tiles_per_b, 0))
    return pl.pallas_call(
        _out_body,
        grid=(n_row_tiles,),
        in_specs=[pl.BlockSpec((bb * ll, d), lambda i: (i, 0)),
                  xspec,
                  pl.BlockSpec((bb, 1, d), lambda i: (i // tiles_per_b, 0, 2)),
                  _resident((1, d)), _resident(w_out_bf.shape)],
        out_specs=xspec,
        out_shape=jax.ShapeDtypeStruct(x3.shape, F32),
        compiler_params=_params(1),
        name="out_projection",
    )(merged2, x3, mod3, g_post.reshape(1, d), w_out_bf)


def _layer(l, yp, ys, cache_k, cache_v, state_l, page_table, c_prompt, c_sample, p):
    f32 = F32
    bp, lp, d = yp.shape
    bs, ls, _ = ys.shape
    d_attn = N_HEADS * HEAD_W
    g, pst = p['lam_re'].shape
    d_ssm = g * SSM_GROUP
    lam_init = 0.8 - 0.6 * math.exp(-0.3 * l)

    n_c = bp + bs
    pad = (-n_c) % SUBLANES
    c_all = jnp.concatenate([c_prompt, c_sample, jnp.zeros((pad, d), f32)], axis=0)
    mod = _modulation(c_all, p['w_ada'], p['b_ada'])
    mod_p = mod[:bp].reshape(bp, 1, 3 * d)
    mod_s = mod[bp:n_c].reshape(bs, 1, 3 * d)

    w_in_bf = p['w_in'].astype(BF16)
    qp, kp, vp, rest_p = _in_projection(yp, mod_p, p['g_pre'], w_in_bf, bb=1, ll=min(lp, 1024), q_dtype=BF16)
    qs, ks, vs, rest_s = _in_projection(ys, mod_s, p['g_pre'], w_in_bf, bb=bs, ll=ls, q_dtype=f32)

    lam_vecs = [p[n].reshape(1, D_HEAD) for n in ('lq1', 'lk1', 'lq2', 'lk2')]
    rest_p3 = rest_p.reshape(bp, lp, -1)
    rest_s3 = rest_s.reshape(bs, ls, -1)
    attn_p = _prompt_attention(qp.reshape(bp, lp, d_attn), kp.reshape(bp, lp, d_attn), vp.reshape(bp, lp, d_attn),
                               rest_p3, lam_vecs, p['g_subln'], lam_init)
    attn_s = _sample_attention(qs.reshape(bs, ls, d_attn), ks.reshape(bs, ls, d_attn), vs.reshape(bs, ls, d_attn),
                               rest_s3, cache_k, cache_v, l, page_table, lam_vecs, p['g_subln'], lam_init)

    lbr, lbi, bbr, bbi = _ssm_prep(p['lam_re'], p['lam_im'], p['log_dt'],
                                   p['b_re'].transpose(0, 2, 1), p['b_im'].transpose(0, 2, 1))
    gpb = 256 // SSM_GROUP
    n_blk = g // gpb
    wb = jnp.concatenate([_blockdiag(bbr.reshape(n_blk, gpb, SSM_GROUP, pst)),
                          _blockdiag(bbi.reshape(n_blk, gpb, SSM_GROUP, pst))], axis=1).astype(BF16)
    wc = jnp.concatenate([_blockdiag(p['c_re'].transpose(0, 2, 1).reshape(n_blk, gpb, pst, SSM_GROUP)),
                          _blockdiag(p['c_im'].transpose(0, 2, 1).reshape(n_blk, gpb, pst, SSM_GROUP))],
                         axis=2).astype(BF16)
    lbr_row = lbr.reshape(1, g * pst)
    lbi_row = lbi.reshape(1, g * pst)
    half = SUBLANES // 2
    a_rows = jnp.broadcast_to(lbr_row, (SUBLANES, g * pst))
    bs_rows = jnp.concatenate([jnp.broadcast_to(-lbi_row, (half, g * pst)),
                               jnp.broadcast_to(lbi_row, (half, g * pst))], axis=0)
    u_blk = d_attn // d_ssm

    y_p, hl_p = _s5_prompt(rest_p3, u_blk, wb, wc, a_rows, bs_rows, p['d_skip'])
    h0 = state_l.astype(f32).reshape(bs, g * pst, 2)
    y_s, hl_sr, hl_si = _s5_sample(rest_s, u_blk, h0[..., 0], h0[..., 1], wb, wc, lbr_row, lbi_row,
                                   p['d_skip'], ls)

    w_glu_bf = p['w_glu'].astype(BF16)
    w_pa_bf = p['w_proj_a'].astype(BF16)
    w_ps_bf = p['w_proj_s'].astype(BF16)
    w_out_bf = p['w_out'].astype(BF16)
    merged_p = _merge(attn_p.reshape(bp * lp, d_attn), y_p.reshape(bp * lp, d_ssm), rest_p,
                      w_glu_bf, p['b_glu'], w_pa_bf, w_ps_bf)
    merged_s = _merge(attn_s.reshape(bs * ls, d_attn), y_s, rest_s, w_glu_bf, p['b_glu'], w_pa_bf, w_ps_bf)
    yp_new = _out_projection(merged_p, yp, mod_p, p['g_post'], w_out_bf, bb=1, ll=min(lp, 512))
    ys_new = _out_projection(merged_s, ys, mod_s, p['g_post'], w_out_bf, bb=min(bs, 64), ll=ls)

    k_p = kp.reshape(bp, lp, N_HEADS, HEAD_W)
    v_p = vp.reshape(bp, lp, N_HEADS, HEAD_W)
    k_s = ks.reshape(bs, ls, N_HEADS, HEAD_W)
    v_s = vs.reshape(bs, ls, N_HEADS, HEAD_W)
    h_p = jnp.stack([hl_p[:half].reshape(bp, g, pst), hl_p[half:].reshape(bp, g, pst)], axis=-1)
    h_s = jnp.stack([hl_sr.reshape(bs, g, pst), hl_si.reshape(bs, g, pst)], axis=-1)
    return yp_new, ys_new, k_p, v_p, h_p, k_s, v_s, h_s


def kernel(x_prompt, x_sample, cache_k, cache_v, state_ssm, page_table, c_prompt, c_sample, w_ada, b_ada, g_pre, g_post, w_in, lambda_q1, lambda_k1, lambda_q2, lambda_k2, g_subln, ssm_lambda_re, ssm_lambda_im, ssm_log_dt, ssm_b_re, ssm_b_im, ssm_c_re, ssm_c_im, ssm_d, w_glu, b_glu, w_proj_a, w_proj_s, w_out):
    depth = w_in.shape[0]
    yp, ys = x_prompt, x_sample
    outs = [[] for _ in range(6)]
    for l in range(depth):
        p = {'w_ada': w_ada[l], 'b_ada': b_ada[l], 'g_pre': g_pre[l], 'g_post': g_post[l],
             'w_in': w_in[l], 'lq1': lambda_q1[l], 'lk1': lambda_k1[l], 'lq2': lambda_q2[l],
             'lk2': lambda_k2[l], 'g_subln': g_subln[l], 'lam_re': ssm_lambda_re[l],
             'lam_im': ssm_lambda_im[l], 'log_dt': ssm_log_dt[l], 'b_re': ssm_b_re[l],
             'b_im': ssm_b_im[l], 'c_re': ssm_c_re[l], 'c_im': ssm_c_im[l], 'd_skip': ssm_d[l],
             'w_glu': w_glu[l], 'b_glu': b_glu[l], 'w_proj_a': w_proj_a[l],
             'w_proj_s': w_proj_s[l], 'w_out': w_out[l]}
        yp, ys, *rest = _layer(l, yp, ys, cache_k, cache_v, state_ssm[l], page_table,
                               c_prompt, c_sample, p)
        for acc, r in zip(outs, rest):
            acc.append(r)
    k_p, v_p, h_p, k_s, v_s, h_s = (jnp.stack(o, axis=0) for o in outs)
    return (yp, ys, k_p, v_p, h_p, k_s, v_s, h_s)
```
